```python
import jax, jax.numpy as jnp
from jax import lax
import numpy as np

D_MODEL = 1024
BATCH = 16
SEQ = 4096
DEPTH = 4

CHUNK = 64
POOL_WINDOWS = (2, 4, 8, 16)
POOL_GROUPS = len(POOL_WINDOWS)
D_POOL = D_MODEL // 2
POOL_GW = D_POOL // POOL_GROUPS
D_CONV = D_MODEL // 2
CONV_WIDTH = 31
D_IN_EVEN = D_POOL + 2 * D_CONV
D_MIX_EVEN = D_POOL + D_CONV
SGU_LEN = 2 * CHUNK
SGU_HEADS = 4
D_SGU = D_MODEL
SGU_HD = D_SGU // SGU_HEADS
D_FF = 2816
FFN_CONV_WIDTH = 3
N_EVEN = (DEPTH + 1) // 2
N_ODD = DEPTH // 2
EPS = 1e-6

kernel_name = "pool_conformer_sgu_convffn_hybrid"


def rms_norm(x, g):
    xf = x.astype(jnp.float32)
    y = xf * lax.rsqrt(jnp.mean(xf * xf, axis=-1, keepdims=True) + EPS)
    return (y * g.astype(jnp.float32)).astype(x.dtype)


def layer_norm(x, g, b):
    xf = x.astype(jnp.float32)
    mu = jnp.mean(xf, axis=-1, keepdims=True)
    xc = xf - mu
    var = jnp.mean(xc * xc, axis=-1, keepdims=True)
    y = xc * lax.rsqrt(var + EPS) * g.astype(jnp.float32) + b.astype(jnp.float32)
    return y.astype(x.dtype)


def causal_dwconv(x, w, b):
    k, c = w.shape
    y = lax.conv_general_dilated(
        x, w.astype(x.dtype)[:, None, :], window_strides=(1,), padding=[(k - 1, 0)],
        dimension_numbers=("NWC", "WIO", "NWC"), feature_group_count=c)
    return y + b.astype(x.dtype)


def multiscale_pool_residual(u):
    s = u.shape[1]
    uf = u.astype(jnp.float32)
    cs = jnp.pad(jnp.cumsum(uf, axis=1), ((0, 0), (1, 0), (0, 0), (0, 0)))
    pos = jnp.arange(s)
    outs = []
    for g, w in enumerate(POOL_WINDOWS):
        c = cs[:, :, g]
        lower = jnp.pad(c, ((0, 0), (w - 1, 0), (0, 0)))[:, :s]
        cnt = jnp.minimum(pos + 1, w).astype(jnp.float32)[None, :, None]
        outs.append((c[:, 1:] - lower) / cnt)
    pooled = jnp.stack(outs, axis=2)
    return (pooled - uf).astype(u.dtype)


def pool_conv_mixer(h, w_in, pool_w, pool_scale, conv_w, conv_b, cn_g, cn_b, w_out):
    bsz, s, _ = h.shape
    z = h @ w_in
    za = z[..., :D_POOL]
    zb_val = z[..., D_POOL:D_POOL + D_CONV]
    zb_gate = z[..., D_POOL + D_CONV:]
    pa = multiscale_pool_residual(za.reshape(bsz, s, POOL_GROUPS, POOL_GW))
    ya = jnp.einsum("bsgc,gcd->bsgd", pa, pool_w).reshape(bsz, s, D_POOL) * pool_scale
    gl = zb_val * jax.nn.sigmoid(zb_gate)
    yb = jax.nn.silu(layer_norm(causal_dwconv(gl, conv_w, conv_b), cn_g, cn_b))
    return jnp.concatenate([ya, yb], axis=-1) @ w_out


def sgu_mixer(h, w_in, vn_g, vn_b, w_s, b_s, w_out):
    bsz, s, _ = h.shape
    z = jax.nn.gelu(h @ w_in, approximate=False)
    u, v = jnp.split(z, 2, axis=-1)
    v = layer_norm(v, vn_g, vn_b)
    n = s // SGU_LEN
    v = v.reshape(bsz, n, SGU_LEN, SGU_HEADS, SGU_HD)
    mask = jnp.tril(jnp.ones((SGU_LEN, SGU_LEN), dtype=bool))
    ws = jnp.where(mask[None], w_s, jnp.zeros_like(w_s)).astype(v.dtype)
    sv = jnp.einsum("hqp,bnphd->bnqhd", ws, v) + b_s.T.astype(v.dtype)[None, None, :, :, None]
    return (u * sv.reshape(bsz, s, D_SGU)) @ w_out


def conv_ffn(h, w_up, conv_w, conv_b, w_down):
    z = causal_dwconv(h @ w_up, conv_w, conv_b)
    a, g = jnp.split(z, 2, axis=-1)
    return (jax.nn.silu(g) * a) @ w_down


def setup_inputs(seed: int = 0) -> dict:
    key = jax.random.key(seed)
    ks = iter(jax.random.split(key, 32))

    def nrm(shape, scale):
        return jax.random.normal(next(ks), shape, jnp.float32) * scale

    def gain(shape):
        return 1.0 + nrm(shape, 0.02)

    return {
        "x": nrm((BATCH, SEQ, D_MODEL), 1.0),
        "ev_w_in": nrm((N_EVEN, D_MODEL, D_IN_EVEN), D_MODEL ** -0.5),
        "ev_pool_w": nrm((N_EVEN, POOL_GROUPS, POOL_GW, POOL_GW), POOL_GW ** -0.5),
        "ev_pool_scale": gain((N_EVEN, D_POOL)),
        "ev_conv_w": nrm((N_EVEN, CONV_WIDTH, D_CONV), CONV_WIDTH ** -0.5),
        "ev_conv_b": nrm((N_EVEN, D_CONV), 0.02),
        "ev_cn_g": gain((N_EVEN, D_CONV)),
        "ev_cn_b": nrm((N_EVEN, D_CONV), 0.02),
        "ev_w_out": nrm((N_EVEN, D_MIX_EVEN, D_MODEL), D_MIX_EVEN ** -0.5),
        "od_w_in": nrm((N_ODD, D_MODEL, 2 * D_SGU), D_MODEL ** -0.5),
        "od_vn_g": gain((N_ODD, D_SGU)),
        "od_vn_b": nrm((N_ODD, D_SGU), 0.02),
        "od_w_s": nrm((N_ODD, SGU_HEADS, SGU_LEN, SGU_LEN), SGU_LEN ** -0.5),
        "od_b_s": nrm((N_ODD, SGU_HEADS, SGU_LEN), 0.02),
        "od_w_out": nrm((N_ODD, D_SGU, D_MODEL), D_SGU ** -0.5),
        "ffn_w_up": nrm((DEPTH, D_MODEL, 2 * D_FF), D_MODEL ** -0.5),
        "ffn_conv_w": nrm((DEPTH, FFN_CONV_WIDTH, 2 * D_FF), FFN_CONV_WIDTH ** -0.5),
        "ffn_conv_b": nrm((DEPTH, 2 * D_FF), 0.02),
        "ffn_w_down": nrm((DEPTH, D_FF, D_MODEL), D_FF ** -0.5),
        "mix_norm_g": gain((DEPTH, D_MODEL)),
        "ffn_norm_g": gain((DEPTH, D_MODEL)),
        "final_norm_g": gain((D_MODEL,)),
    }


def reference(x, ev_w_in, ev_pool_w, ev_pool_scale, ev_conv_w, ev_conv_b, ev_cn_g, ev_cn_b,
              ev_w_out, od_w_in, od_vn_g, od_vn_b, od_w_s, od_b_s, od_w_out,
              ffn_w_up, ffn_conv_w, ffn_conv_b, ffn_w_down,
              mix_norm_g, ffn_norm_g, final_norm_g):
    for l in range(DEPTH):
        h = rms_norm(x, mix_norm_g[l])
        i = l // 2
        if l % 2 == 0:
            x = x + pool_conv_mixer(h, ev_w_in[i], ev_pool_w[i], ev_pool_scale[i],
                                    ev_conv_w[i], ev_conv_b[i], ev_cn_g[i], ev_cn_b[i],
                                    ev_w_out[i])
        else:
            x = x + sgu_mixer(h, od_w_in[i], od_vn_g[i], od_vn_b[i], od_w_s[i], od_b_s[i],
                              od_w_out[i])
        x = x + conv_ffn(rms_norm(x, ffn_norm_g[l]), ffn_w_up[l], ffn_conv_w[l],
                         ffn_conv_b[l], ffn_w_down[l])
    return rms_norm(x, final_norm_g)
```

```python
import functools

import jax
import jax.numpy as jnp
from jax import lax
from jax.experimental import pallas as pl
from jax.experimental.pallas import tpu as pltpu

EPS = 1e-6
POOL_WINDOWS = (2, 4, 8, 16)

V7X_SUBLANES = 8
V7X_LANES = 128
V7X_BF16_ROWS = 16
V7X_MXU_DIM = 256
V7X_VMEM_LIMIT_BYTES = 60000 * 1024

SEQ_TILE = 512
FFN_HALO = V7X_BF16_ROWS
EVEN_HALO = 2 * V7X_BF16_ROWS
FFN_CHUNK = V7X_MXU_DIM

F32 = jnp.float32
BF16 = jnp.bfloat16


def _rms(x, g):
    return x * lax.rsqrt(jnp.mean(x * x, axis=-1, keepdims=True) + EPS) * g


def _layer_norm(x, g, b):
    mu = jnp.mean(x, axis=-1, keepdims=True)
    xc = x - mu
    var = jnp.mean(xc * xc, axis=-1, keepdims=True)
    return xc * lax.rsqrt(var + EPS) * g + b


def _sigmoid(x):
    return 1.0 / (1.0 + jnp.exp(-x))


def _shift_rows(x, k):
    return pltpu.roll(x, k, 0)


def _with_history(xp_ref, x, s):
    prev = jnp.where(s > 0, xp_ref[...], 0.0)
    return jnp.concatenate([prev, x], axis=0)


def _ffn_body(x_ref, xp_ref, g_ref, wup_ref, cw_ref, cb_ref, wdn_ref, fg_ref,
              o_ref, y_ref, *, final_norm):
    s = pl.program_id(1)
    halo = xp_ref.shape[0]
    d_ff = y_ref.shape[1]
    x = x_ref[...]
    h = _rms(_with_history(xp_ref, x, s), g_ref[...]).astype(BF16)
    two_c = 2 * FFN_CHUNK
    for c in range(d_ff // FFN_CHUNK):
        cols = slice(c * two_c, (c + 1) * two_c)
        z = jnp.dot(h, wup_ref[:, cols], preferred_element_type=F32)
        w = cw_ref[:, cols]
        zc = (z * w[2:3] + _shift_rows(z, 1) * w[1:2] + _shift_rows(z, 2) * w[0:1]
              + cb_ref[:, cols])[halo:]
        a = zc[:, :FFN_CHUNK]
        gt = zc[:, FFN_CHUNK:]
        y_ref[:, c * FFN_CHUNK:(c + 1) * FFN_CHUNK] = (gt * _sigmoid(gt) * a).astype(BF16)
    out = x + jnp.dot(y_ref[...], wdn_ref[...], preferred_element_type=F32)
    if final_norm:
        out = _rms(out, fg_ref[...])
    o_ref[...] = out


def _even_body(x_ref, xp_ref, g_ref, win_ref, pw_ref, ps_ref, cw_ref, cb_ref,
               lng_ref, lnb_ref, wout_ref, o_ref):
    s = pl.program_id(1)
    t = x_ref.shape[0]
    halo = xp_ref.shape[0]
    n_groups, gw, _ = pw_ref.shape
    d_pool = n_groups * gw
    conv_k, d_conv = cw_ref.shape
    x = x_ref[...]
    h = _rms(_with_history(xp_ref, x, s), g_ref[...]).astype(BF16)
    z = jnp.dot(h, win_ref[...], preferred_element_type=F32)

    pos = s * t + lax.broadcasted_iota(jnp.int32, (t, 1), 0)
    ya = []
    for gi, win in enumerate(POOL_WINDOWS):
        u = z[:, gi * gw:(gi + 1) * gw]
        acc, span = u, 1
        while span < win:
            acc = acc + _shift_rows(acc, span)
            span *= 2
        inv_cnt = 1.0 / jnp.minimum(pos + 1, win).astype(F32)
        pa = acc[halo:] * inv_cnt - u[halo:]
        yg = jnp.dot(pa.astype(BF16), pw_ref[gi], preferred_element_type=F32)
        ya.append(yg * ps_ref[:, gi * gw:(gi + 1) * gw])

    gl = z[:, d_pool:d_pool + d_conv] * _sigmoid(z[:, d_pool + d_conv:])
    shifted = [gl] + [_shift_rows(gl, r) for r in range(1, V7X_SUBLANES)]
    acc = None
    for k in range(conv_k):
        blk, r = divmod(conv_k - 1 - k, V7X_SUBLANES)
        lo = halo - V7X_SUBLANES * blk
        term = shifted[r][lo:lo + t] * cw_ref[k:k + 1, :]
        acc = term if acc is None else acc + term
    cv = _layer_norm(acc + cb_ref[...], lng_ref[...], lnb_ref[...])
    yb = cv * _sigmoid(cv)

    mix = jnp.concatenate(ya + [yb], axis=1).astype(BF16)
    o_ref[...] = x + jnp.dot(mix, wout_ref[...], preferred_element_type=F32)


def _odd_body(x_ref, g_ref, win_ref, vng_ref, vnb_ref, ws_ref, bs_ref, wout_ref, o_ref):
    t = x_ref.shape[0]
    n_heads, sgu_len, _ = ws_ref.shape
    d_sgu = vng_ref.shape[1]
    hd = d_sgu // n_heads
    x = x_ref[...]
    h = _rms(x, g_ref[...]).astype(BF16)
    z = jnp.dot(h, win_ref[...], preferred_element_type=F32)
    z = 0.5 * z * (1.0 + lax.erf(z * (2.0 ** -0.5)))
    u = z[:, :d_sgu]
    v = _layer_norm(z[:, d_sgu:], vng_ref[...], vnb_ref[...]).astype(BF16)
    q_idx = lax.broadcasted_iota(jnp.int32, (sgu_len, sgu_len), 0)
    p_idx = lax.broadcasted_iota(jnp.int32, (sgu_len, sgu_len), 1)
    heads = []
    for hi in range(n_heads):
        ws = jnp.where(p_idx <= q_idx, ws_ref[hi], 0.0).astype(BF16)
        bias = bs_ref[:, hi:hi + 1]
        rows = []
        for n in range(t // sgu_len):
            vc = v[n * sgu_len:(n + 1) * sgu_len, hi * hd:(hi + 1) * hd]
            rows.append(jnp.dot(ws, vc, preferred_element_type=F32) + bias)
        heads.append(jnp.concatenate(rows, axis=0))
    sv = jnp.concatenate(heads, axis=1)
    o_ref[...] = x + jnp.dot((u * sv).astype(BF16), wout_ref[...], preferred_element_type=F32)


def _resident(shape):
    zeros = (0,) * len(shape)
    return pl.BlockSpec(shape, lambda b, s: zeros, pipeline_mode=pl.Buffered(1))


def _tile_spec(t, d):
    return pl.BlockSpec((None, t, d), lambda b, s: (b, s, 0))


def _history_spec(t, halo, d):
    per_tile = t // halo
    return pl.BlockSpec((None, halo, d), lambda b, s: (b, jnp.maximum(s * per_tile - 1, 0), 0))


def _call(body, x, in_specs, args, scratch_shapes=(), name=None):
    bsz, seq, d = x.shape
    return pl.pallas_call(
        body,
        grid=(bsz, seq // SEQ_TILE),
        in_specs=in_specs,
        out_specs=_tile_spec(SEQ_TILE, d),
        out_shape=jax.ShapeDtypeStruct(x.shape, x.dtype),
        scratch_shapes=list(scratch_shapes),
        compiler_params=pltpu.CompilerParams(
            dimension_semantics=("parallel", "parallel"),
            vmem_limit_bytes=V7X_VMEM_LIMIT_BYTES),
        name=name,
    )(*args)


def _row(v):
    return v.reshape(1, -1).astype(F32)


def _conv_ffn(x, norm_g, w_up, conv_w, conv_b, w_down, final_g, final_norm):
    d = x.shape[-1]
    d_ff = w_down.shape[0]
    assert d_ff % FFN_CHUNK == 0 and x.shape[1] % SEQ_TILE == 0 and SEQ_TILE % FFN_HALO == 0
    assert conv_w.shape[0] - 1 <= FFN_HALO

    def interleave(a):
        lead = a.shape[:-1]
        a = a.reshape(lead + (2, d_ff // FFN_CHUNK, FFN_CHUNK))
        return jnp.swapaxes(a, -3, -2).reshape(lead + (2 * d_ff,))

    args = (x, x, _row(norm_g), interleave(w_up).astype(BF16), interleave(conv_w).astype(F32),
            interleave(conv_b).reshape(1, -1).astype(F32), w_down.astype(BF16), _row(final_g))
    in_specs = [_tile_spec(SEQ_TILE, d), _history_spec(SEQ_TILE, FFN_HALO, d)]
    in_specs += [_resident(a.shape) for a in args[2:]]
    return _call(functools.partial(_ffn_body, final_norm=final_norm), x, in_specs, args,
                 scratch_shapes=[pltpu.VMEM((SEQ_TILE, d_ff), BF16)], name="conv_ffn")


def _even_mixer(x, norm_g, w_in, pool_w, pool_scale, conv_w, conv_b, cn_g, cn_b, w_out):
    d = x.shape[-1]
    assert x.shape[1] % SEQ_TILE == 0 and SEQ_TILE % EVEN_HALO == 0
    assert conv_w.shape[0] - 1 <= EVEN_HALO and max(POOL_WINDOWS) - 1 <= EVEN_HALO
    assert conv_w.shape[0] - 1 < 4 * V7X_SUBLANES <= EVEN_HALO
    args = (x, x, _row(norm_g), w_in.astype(BF16), pool_w.astype(BF16), _row(pool_scale),
            conv_w.astype(F32), _row(conv_b), _row(cn_g), _row(cn_b), w_out.astype(BF16))
    in_specs = [_tile_spec(SEQ_TILE, d), _history_spec(SEQ_TILE, EVEN_HALO, d)]
    in_specs += [_resident(a.shape) for a in args[2:]]
    return _call(_even_body, x, in_specs, args, name="even_mixer")


def _odd_mixer(x, norm_g, w_in, vn_g, vn_b, w_s, b_s, w_out):
    d = x.shape[-1]
    assert x.shape[1] % SEQ_TILE == 0 and SEQ_TILE % w_s.shape[-1] == 0
    args = (x, _row(norm_g), w_in.astype(BF16), _row(vn_g), _row(vn_b), w_s.astype(F32),
            b_s.T.astype(F32), w_out.astype(BF16))
    in_specs = [_tile_spec(SEQ_TILE, d)] + [_resident(a.shape) for a in args[1:]]
    return _call(_odd_body, x, in_specs, args, name="odd_mixer")


def kernel(x, ev_w_in, ev_pool_w, ev_pool_scale, ev_conv_w, ev_conv_b, ev_cn_g, ev_cn_b, ev_w_out, od_w_in, od_vn_g, od_vn_b, od_w_s, od_b_s, od_w_out, ffn_w_up, ffn_conv_w, ffn_conv_b, ffn_w_down, mix_norm_g, ffn_norm_g, final_norm_g):
    depth = ffn_w_up.shape[0]
    for l in range(depth):
        i = l // 2
        if l % 2 == 0:
            x = _even_mixer(x, mix_norm_g[l], ev_w_in[i], ev_pool_w[i], ev_pool_scale[i],
                            ev_conv_w[i], ev_conv_b[i], ev_cn_g[i], ev_cn_b[i], ev_w_out[i])
        else:
            x = _odd_mixer(x, mix_norm_g[l], od_w_in[i], od_vn_g[i], od_vn_b[i], od_w_s[i],
                           od_b_s[i], od_w_out[i])
        x = _conv_ffn(x, ffn_norm_g[l], ffn_w_up[l], ffn_conv_w[l], ffn_conv_b[l],
                      ffn_w_down[l], final_norm_g, final_norm=(l == depth - 1))
    return x
```
